```python
import jax, jax.numpy as jnp
from jax import lax
import numpy as np

D_MODEL = 2048
BATCH = 4
SEQ = 2048
DEPTH = 4
DEC_BATCH = 8
DEC_SEQ = 8
PAST_LEN = 16384
PAGE_SIZE = 128

N_MIXERS = 2
N_CONV_LAYERS = (DEPTH + 1) // 2
N_ATTN_LAYERS = DEPTH // 2
CONV_WIDTH = 31
CONV_STATE = CONV_WIDTH - 1
WINDOWS = (128, 512, 2048)
DILATIONS = (1, 4, 16)
N_GROUPS = 3
HEAD_DIM = 128
HEADS_PER_GROUP = D_MODEL // (2 * HEAD_DIM)
GROUP_WIDTH = HEADS_PER_GROUP * HEAD_DIM
QKV_WIDTH = N_GROUPS * 3 * GROUP_WIDTH
D_FF = 4 * D_MODEL
RMS_EPS = 1e-6
LN_EPS = 1e-5
NEG_INF = -1e30
SCALE = HEAD_DIM ** -0.5

kernel_name = "conformer_conv_dilated_swa_hybrid_step"


def rms_norm(x, g):
    xf = x.astype(jnp.float32)
    y = xf * lax.rsqrt(jnp.mean(xf * xf, axis=-1, keepdims=True) + RMS_EPS)
    return (y * g.astype(jnp.float32)).astype(x.dtype)


def sqrelu_mlp(h, w1, w2):
    a = jnp.maximum(h @ w1, 0)
    return (a * a) @ w2


def conv_module(h, conv_buf, w_pw1, b_pw1, w_dw, b_dw, ln_g, ln_b, w_pw2, b_pw2):
    u = h @ w_pw1 + b_pw1
    a, gate = jnp.split(u, 2, axis=-1)
    glu = a * jax.nn.sigmoid(gate)
    padded = jnp.concatenate([conv_buf.astype(glu.dtype), glu], axis=1)
    c = lax.conv_general_dilated(
        padded, w_dw[:, None, :].astype(padded.dtype), window_strides=(1,),
        padding='VALID', dimension_numbers=('NWC', 'WIO', 'NWC'),
        feature_group_count=D_MODEL) + b_dw
    cf = c.astype(jnp.float32)
    mu = jnp.mean(cf, axis=-1, keepdims=True)
    var = jnp.mean(jnp.square(cf - mu), axis=-1, keepdims=True)
    n = (cf - mu) * lax.rsqrt(var + LN_EPS) * ln_g.astype(jnp.float32) + ln_b.astype(jnp.float32)
    s = (n * jax.nn.sigmoid(n)).astype(h.dtype)
    out = s @ w_pw2 + b_pw2
    return out, padded[:, -CONV_STATE:]


def dilated_group_prompt(q, k, v, window, dilation):
    B, T, H, E = q.shape
    band = window // dilation
    span = band * dilation
    t_pad = -(-T // span) * span
    nb = t_pad // span
    pad = ((0, 0), (0, t_pad - T), (0, 0), (0, 0))

    def to_blocks(a):
        return jnp.pad(a, pad).reshape(B, nb, band, dilation, H, E)

    def banded(a):
        ap = jnp.pad(a, ((0, 0), (1, 0), (0, 0), (0, 0), (0, 0), (0, 0)))
        return jnp.concatenate([ap[:, :-1], ap[:, 1:]], axis=2)

    qb = to_blocks(q)
    kk = banded(to_blocks(k))
    vv = banded(to_blocks(v))
    s = jnp.einsum('bnqrhe,bnkrhe->bhrnqk', qb, kk, preferred_element_type=jnp.float32)
    qi = jnp.arange(band)[:, None]
    ki = jnp.arange(2 * band)[None, :]
    dist = band + qi - ki
    blk = jnp.arange(nb)[:, None, None]
    valid = (dist >= 0) & (dist <= band) & (blk * band + ki - band >= 0)
    s = jnp.where(valid, s, NEG_INF)
    lse = jax.nn.logsumexp(s, axis=-1, keepdims=True)
    p = jnp.exp(s - lse)
    o = jnp.einsum('bhrnqk,bnkrhe->bnqrhe', p.astype(vv.dtype), vv,
                   preferred_element_type=jnp.float32)
    o = o.reshape(B, t_pad, H, E)[:, :T]
    lse = lse[..., 0].transpose(0, 3, 4, 2, 1).reshape(B, t_pad, H)[:, :T]
    return o, lse


def dilated_group_sample(q, k_cat, v_cat, n_past, window, dilation):
    S = q.shape[1]
    n_keys = window // dilation + 1
    idx = n_past + jnp.arange(S)[:, None] - dilation * jnp.arange(n_keys)[None, :]
    valid = idx >= 0
    idx_c = jnp.maximum(idx, 0)
    kg = k_cat[:, idx_c]
    vg = v_cat[:, idx_c]
    s = jnp.einsum('bshe,bsnhe->bhsn', q, kg, preferred_element_type=jnp.float32)
    s = jnp.where(valid, s, NEG_INF)
    lse = jax.nn.logsumexp(s, axis=-1, keepdims=True)
    p = jnp.exp(s - lse)
    o = jnp.einsum('bhsn,bsnhe->bshe', p.astype(vg.dtype), vg,
                   preferred_element_type=jnp.float32)
    return o, lse[..., 0].transpose(0, 2, 1)


def merge_groups(outs, lses, dtype):
    w = jax.nn.softmax(jnp.stack(lses), axis=0)
    return jnp.sum(w[..., None] * jnp.stack(outs), axis=0).astype(dtype)


def split_qkv(h, w_qkv):
    B, T, _ = h.shape
    return (h @ w_qkv).reshape(B, T, N_GROUPS, 3, HEADS_PER_GROUP, HEAD_DIM)


def attn_prompt(h, w_qkv, w_o):
    B, T, _ = h.shape
    qkv = split_qkv(h, w_qkv)
    outs, lses, new_kv = [], [], []
    for g in range(N_GROUPS):
        q = qkv[:, :, g, 0] * SCALE
        k = qkv[:, :, g, 1]
        v = qkv[:, :, g, 2]
        o, l = dilated_group_prompt(q, k, v, WINDOWS[g], DILATIONS[g])
        outs.append(o)
        lses.append(l)
        keep = min(WINDOWS[g], T)
        new_kv.append(jnp.stack([k[:, T - keep:], v[:, T - keep:]], axis=2))
    o = merge_groups(outs, lses, h.dtype).reshape(B, T, GROUP_WIDTH)
    return o @ w_o, new_kv


def attn_sample(h, kv_bufs, w_qkv, w_o):
    B, S, _ = h.shape
    qkv = split_qkv(h, w_qkv)
    outs, lses, new_kv = [], [], []
    for g in range(N_GROUPS):
        buf = kv_bufs[g]
        q = qkv[:, :, g, 0] * SCALE
        k = qkv[:, :, g, 1]
        v = qkv[:, :, g, 2]
        k_cat = jnp.concatenate([buf[:, :, 0].astype(k.dtype), k], axis=1)
        v_cat = jnp.concatenate([buf[:, :, 1].astype(v.dtype), v], axis=1)
        o, l = dilated_group_sample(q, k_cat, v_cat, buf.shape[1], WINDOWS[g], DILATIONS[g])
        outs.append(o)
        lses.append(l)
        new_kv.append(jnp.stack([k, v], axis=2))
    o = merge_groups(outs, lses, h.dtype).reshape(B, S, GROUP_WIDTH)
    return o @ w_o, new_kv


def setup_inputs(seed: int = 0) -> dict:
    key = jax.random.key(seed)
    ks = jax.random.split(key, 24)
    f32 = jnp.float32
    nrm = lambda k, shape, scale=1.0: jax.random.normal(k, shape, f32) * scale
    cache_lens = [min(w, PAST_LEN) for w in WINDOWS]
    kv_shape = lambda L: (N_ATTN_LAYERS, DEC_BATCH, L, 2, HEADS_PER_GROUP, HEAD_DIM)
    return {
        'x_prompt': nrm(ks[0], (BATCH, SEQ, D_MODEL)),
        'x_sample': nrm(ks[1], (DEC_BATCH, DEC_SEQ, D_MODEL)),
        'state_conv': nrm(ks[2], (N_CONV_LAYERS, DEC_BATCH, CONV_STATE, D_MODEL), 0.5),
        'cache_kv_w128': nrm(ks[3], kv_shape(cache_lens[0])),
        'cache_kv_w512': nrm(ks[4], kv_shape(cache_lens[1])),
        'cache_kv_w2048': nrm(ks[5], kv_shape(cache_lens[2])),
        'norm_mix': 1.0 + nrm(ks[6], (DEPTH, D_MODEL), 0.01),
        'norm_mlp': 1.0 + nrm(ks[7], (DEPTH, D_MODEL), 0.01),
        'norm_final': 1.0 + nrm(ks[8], (D_MODEL,), 0.01),
        'conv_w_pw1': nrm(ks[9], (N_CONV_LAYERS, D_MODEL, 2 * D_MODEL), D_MODEL ** -0.5),
        'conv_b_pw1': nrm(ks[10], (N_CONV_LAYERS, 2 * D_MODEL), 0.01),
        'conv_w_dw': nrm(ks[11], (N_CONV_LAYERS, CONV_WIDTH, D_MODEL), CONV_WIDTH ** -0.5),
        'conv_b_dw': nrm(ks[12], (N_CONV_LAYERS, D_MODEL), 0.01),
        'conv_ln_g': 1.0 + nrm(ks[13], (N_CONV_LAYERS, D_MODEL), 0.01),
        'conv_ln_b': nrm(ks[14], (N_CONV_LAYERS, D_MODEL), 0.01),
        'conv_w_pw2': nrm(ks[15], (N_CONV_LAYERS, D_MODEL, D_MODEL), D_MODEL ** -0.5),
        'conv_b_pw2': nrm(ks[16], (N_CONV_LAYERS, D_MODEL), 0.01),
        'attn_w_qkv': nrm(ks[17], (N_ATTN_LAYERS, D_MODEL, QKV_WIDTH), D_MODEL ** -0.5),
        'attn_w_o': nrm(ks[18], (N_ATTN_LAYERS, GROUP_WIDTH, D_MODEL), GROUP_WIDTH ** -0.5),
        'mlp_w1': nrm(ks[19], (DEPTH, D_MODEL, D_FF), D_MODEL ** -0.5),
        'mlp_w2': nrm(ks[20], (DEPTH, D_FF, D_MODEL), D_FF ** -0.5),
    }


def reference(x_prompt, x_sample, state_conv, cache_kv_w128, cache_kv_w512, cache_kv_w2048,
              norm_mix, norm_mlp, norm_final,
              conv_w_pw1, conv_b_pw1, conv_w_dw, conv_b_dw, conv_ln_g, conv_ln_b,
              conv_w_pw2, conv_b_pw2, attn_w_qkv, attn_w_o, mlp_w1, mlp_w2):
    yp, ys = x_prompt, x_sample
    conv_p, conv_s = [], []
    kv_p = [[] for _ in range(N_GROUPS)]
    kv_s = [[] for _ in range(N_GROUPS)]
    for i in range(DEPTH):
        j = i // N_MIXERS
        hp = rms_norm(yp, norm_mix[i])
        hs = rms_norm(ys, norm_mix[i])
        if i % N_MIXERS == 0:
            params = (conv_w_pw1[j], conv_b_pw1[j], conv_w_dw[j], conv_b_dw[j],
                      conv_ln_g[j], conv_ln_b[j], conv_w_pw2[j], conv_b_pw2[j])
            zero_buf = jnp.zeros((hp.shape[0], CONV_STATE, D_MODEL), hp.dtype)
            mp, stp = conv_module(hp, zero_buf, *params)
            ms, sts = conv_module(hs, state_conv[j], *params)
            conv_p.append(stp)
            conv_s.append(sts)
        else:
            bufs = (cache_kv_w128[j], cache_kv_w512[j], cache_kv_w2048[j])
            mp, nkp = attn_prompt(hp, attn_w_qkv[j], attn_w_o[j])
            ms, nks = attn_sample(hs, bufs, attn_w_qkv[j], attn_w_o[j])
            for g in range(N_GROUPS):
                kv_p[g].append(nkp[g])
                kv_s[g].append(nks[g])
        yp = yp + mp
        ys = ys + ms
        yp = yp + sqrelu_mlp(rms_norm(yp, norm_mlp[i]), mlp_w1[i], mlp_w2[i])
        ys = ys + sqrelu_mlp(rms_norm(ys, norm_mlp[i]), mlp_w1[i], mlp_w2[i])
    y_prompt = rms_norm(yp, norm_final)
    y_sample = rms_norm(ys, norm_final)
    return (y_prompt, y_sample,
            jnp.stack(conv_p), jnp.stack(conv_s),
            jnp.stack(kv_p[0]), jnp.stack(kv_s[0]),
            jnp.stack(kv_p[1]), jnp.stack(kv_s[1]),
            jnp.stack(kv_p[2]), jnp.stack(kv_s[2]))
```

```python
import functools

import jax
import jax.numpy as jnp
from jax import lax
from jax.experimental import pallas as pl
from jax.experimental.pallas import tpu as pltpu

D_MODEL = 2048
BATCH = 4
SEQ = 2048
DEPTH = 4
DEC_BATCH = 8
DEC_SEQ = 8
CONV_WIDTH = 31
CONV_STATE = CONV_WIDTH - 1
WINDOWS = (128, 512, 2048)
DILATIONS = (1, 4, 16)
N_GROUPS = 3
HEAD_DIM = 128
HEADS = 8
GROUP_WIDTH = HEADS * HEAD_DIM
QKV_WIDTH = N_GROUPS * 3 * GROUP_WIDTH
N_QKV_HEADS = QKV_WIDTH // HEAD_DIM
D_FF = 4 * D_MODEL
RMS_EPS = 1e-6
LN_EPS = 1e-5
NEG_INF = -1e30
SCALE = HEAD_DIM ** -0.5
BAND = 128

NP_ROWS = BATCH * SEQ
NS_ROWS = DEC_BATCH * DEC_SEQ
TM = 1024
NT = NP_ROWS // TM
TM_CONV = 512
HALO = 32
V7X_VMEM_LIMIT = 56 * 1024 * 1024

BF16 = jnp.bfloat16
F32 = jnp.float32


def _params(sem):
    return pltpu.CompilerParams(dimension_semantics=sem, vmem_limit_bytes=V7X_VMEM_LIMIT)


def _rms(x, g):
    ms = jnp.mean(x * x, axis=-1, keepdims=True)
    return x * lax.rsqrt(ms + RMS_EPS) * g


def _dot(a, b):
    return jnp.dot(a, b, preferred_element_type=F32)


def _sample_col(i, n):
    return jnp.where(i == NT - 1, n, 0)


def _pw1_glu_kernel(xp_ref, xs_ref, g_ref, wa_ref, wg_ref, ba_ref, bg_ref, op_ref, os_ref, hp_ref):
    i = pl.program_id(0)
    n = pl.program_id(1)

    @pl.when(n == 0)
    def _():
        hp_ref[...] = _rms(xp_ref[...], g_ref[...]).astype(BF16)

    def glu(h):
        a = _dot(h, wa_ref[...]) + ba_ref[...]
        gate = _dot(h, wg_ref[...]) + bg_ref[...]
        return a * jax.nn.sigmoid(gate)

    op_ref[...] = glu(hp_ref[...])

    @pl.when(i == NT - 1)
    def _():
        os_ref[...] = glu(_rms(xs_ref[...], g_ref[...]).astype(BF16))


def _pw1_glu(xp, xs, gain, w_pw1, b_pw1, layer, tn=512):
    nn = D_MODEL // tn
    return pl.pallas_call(
        _pw1_glu_kernel,
        grid=(NT, nn),
        in_specs=[
            pl.BlockSpec((TM, D_MODEL), lambda i, n: (i, 0)),
            pl.BlockSpec((NS_ROWS, D_MODEL), lambda i, n: (0, 0)),
            pl.BlockSpec((1, D_MODEL), lambda i, n: (0, 0)),
            pl.BlockSpec((None, D_MODEL, tn), lambda i, n: (layer, 0, n)),
            pl.BlockSpec((None, D_MODEL, tn), lambda i, n: (layer, 0, n + nn)),
            pl.BlockSpec((None, 1, tn), lambda i, n: (layer, 0, n)),
            pl.BlockSpec((None, 1, tn), lambda i, n: (layer, 0, n + nn)),
        ],
        out_specs=[
            pl.BlockSpec((TM, tn), lambda i, n: (i, n)),
            pl.BlockSpec((NS_ROWS, tn), lambda i, n: (0, _sample_col(i, n))),
        ],
        out_shape=[
            jax.ShapeDtypeStruct((NP_ROWS, D_MODEL), F32),
            jax.ShapeDtypeStruct((NS_ROWS, D_MODEL), F32),
        ],
        scratch_shapes=[pltpu.VMEM((TM, D_MODEL), BF16)],
        compiler_params=_params(("arbitrary", "arbitrary")),
        name="pw1_glu",
    )(xp, xs, gain, w_pw1, w_pw1, b_pw1, b_pw1)


CONV_LANES = 256
CONV_ROWS = 128
LN_ROWS = 64


def _depthwise_rows(window, w_ref, lanes):
    rows = CONV_ROWS + 8
    acc = None
    for r in range(8):
        part = None
        for q in range(4):
            s = 8 * q + r
            if s > CONV_STATE:
                continue
            tap = w_ref[pl.ds(CONV_STATE - s, 1), lanes]
            term = tap * window[HALO - 8 - 8 * q:HALO - 8 - 8 * q + rows, :]
            part = term if part is None else part + term
        if r:
            part = pltpu.roll(part, r, 0)
        acc = part if acc is None else acc + part
    return acc[8:, :]


def _ln_swish(c, g, b):
    mu = jnp.mean(c, axis=-1, keepdims=True)
    cc = c - mu
    var = jnp.mean(cc * cc, axis=-1, keepdims=True)
    y = cc * lax.rsqrt(var + LN_EPS) * g + b
    return y * jax.nn.sigmoid(y)


def _conv_prompt_kernel(glu_ref, prev_ref, x_ref, wdw_ref, bdw_ref, lng_ref, lnb_ref, w2_ref, b2_ref,
                        o_ref, c_ref, s_ref):
    i = pl.program_id(0)
    seq_start = (i % (SEQ // TM_CONV)) == 0

    def lane_chunk(c, carry):
        lanes = pl.ds(pl.multiple_of(c * CONV_LANES, CONV_LANES), CONV_LANES)
        hist = jnp.where(seq_start, 0.0, prev_ref[:, lanes])
        bias = bdw_ref[:, lanes]
        for rc in range(TM_CONV // CONV_ROWS):
            t0 = rc * CONV_ROWS
            if rc == 0:
                window = jnp.concatenate([hist, glu_ref[0:CONV_ROWS, lanes]], axis=0)
            else:
                window = glu_ref[t0 - HALO:t0 + CONV_ROWS, lanes]
            c_ref[t0:t0 + CONV_ROWS, lanes] = _depthwise_rows(window, wdw_ref, lanes) + bias
        return carry

    lax.fori_loop(0, D_MODEL // CONV_LANES, lane_chunk, 0)

    def ln_chunk(rc, carry):
        rows = pl.ds(pl.multiple_of(rc * LN_ROWS, LN_ROWS), LN_ROWS)
        s_ref[rows, :] = _ln_swish(c_ref[rows, :], lng_ref[...], lnb_ref[...]).astype(BF16)
        return carry

    lax.fori_loop(0, TM_CONV // LN_ROWS, ln_chunk, 0)
    o_ref[...] = x_ref[...] + _dot(s_ref[...], w2_ref[...]) + b2_ref[...]


def _conv_prompt(glu_p, xp, w_dw, b_dw, ln_g, ln_b, w_pw2, b_pw2, layer):
    halo_blocks = TM_CONV // HALO
    vec = pl.BlockSpec((None, 1, D_MODEL), lambda i: (layer, 0, 0))
    return pl.pallas_call(
        _conv_prompt_kernel,
        grid=(NP_ROWS // TM_CONV,),
        in_specs=[
            pl.BlockSpec((TM_CONV, D_MODEL), lambda i: (i, 0)),
            pl.BlockSpec((HALO, D_MODEL), lambda i: (jnp.maximum(i * halo_blocks - 1, 0), 0)),
            pl.BlockSpec((TM_CONV, D_MODEL), lambda i: (i, 0)),
            pl.BlockSpec((None, CONV_WIDTH, D_MODEL), lambda i: (layer, 0, 0)),
            vec, vec, vec,
            pl.BlockSpec((None, D_MODEL, D_MODEL), lambda i: (layer, 0, 0),
                         pipeline_mode=pl.Buffered(1)),
            vec,
        ],
        out_specs=pl.BlockSpec((TM_CONV, D_MODEL), lambda i: (i, 0)),
        out_shape=jax.ShapeDtypeStruct((NP_ROWS, D_MODEL), F32),
        scratch_shapes=[pltpu.VMEM((TM_CONV, D_MODEL), F32), pltpu.VMEM((TM_CONV, D_MODEL), BF16)],
        compiler_params=_params(("arbitrary",)),
        name="conv_prompt",
    )(glu_p, glu_p, xp, w_dw, b_dw, ln_g, ln_b, w_pw2, b_pw2)


PAD_ROWS = 40


def _conv_sample_kernel(glu_ref, st_ref, x_ref, wdw_ref, bdw_ref, lng_ref, lnb_ref, w2_ref, b2_ref,
                        o_ref, nst_ref, pad_ref, c_ref):
    for b in range(DEC_BATCH):
        pad_ref[0:CONV_STATE, :] = st_ref[b]
        pad_ref[CONV_STATE:CONV_STATE + DEC_SEQ, :] = glu_ref[b * DEC_SEQ:(b + 1) * DEC_SEQ, :]
        acc = None
        for k in range(CONV_WIDTH):
            term = wdw_ref[pl.ds(k, 1), :] * pad_ref[pl.ds(k, DEC_SEQ), :]
            acc = term if acc is None else acc + term
        c_ref[b * DEC_SEQ:(b + 1) * DEC_SEQ, :] = acc + bdw_ref[...]
        nst_ref[b] = pad_ref[DEC_SEQ:DEC_SEQ + CONV_STATE, :]
    s = _ln_swish(c_ref[...], lng_ref[...], lnb_ref[...]).astype(BF16)
    o_ref[...] = x_ref[...] + _dot(s, w2_ref[...]) + b2_ref[...]


def _conv_sample(glu_s, state, xs, w_dw, b_dw, ln_g, ln_b, w_pw2, b_pw2, layer):
    vec = pl.BlockSpec((None, 1, D_MODEL), lambda i: (layer, 0, 0))
    rows = pl.BlockSpec((NS_ROWS, D_MODEL), lambda i: (0, 0))
    st = pl.BlockSpec((None, DEC_BATCH, CONV_STATE, D_MODEL), lambda i: (layer, 0, 0, 0))
    return pl.pallas_call(
        _conv_sample_kernel,
        grid=(1,),
        in_specs=[
            rows, st, rows,
            pl.BlockSpec((None, CONV_WIDTH, D_MODEL), lambda i: (layer, 0, 0)),
            vec, vec, vec,
            pl.BlockSpec((None, D_MODEL, D_MODEL), lambda i: (layer, 0, 0)),
            vec,
        ],
        out_specs=[rows, pl.BlockSpec((DEC_BATCH, CONV_STATE, D_MODEL), lambda i: (0, 0, 0))],
        out_shape=[
            jax.ShapeDtypeStruct((NS_ROWS, D_MODEL), F32),
            jax.ShapeDtypeStruct((DEC_BATCH, CONV_STATE, D_MODEL), F32),
        ],
        scratch_shapes=[pltpu.VMEM((PAD_ROWS, D_MODEL), F32), pltpu.VMEM((NS_ROWS, D_MODEL), F32)],
        compiler_params=_params(("arbitrary",)),
        name="conv_sample",
    )(glu_s, state, xs, w_dw, b_dw, ln_g, ln_b, w_pw2, b_pw2)


def _mlp_kernel(xp_ref, xs_ref, g_ref, w1_ref, w2_ref, gf_ref, op_ref, os_ref, hp_ref, hs_ref, *, final):
    i = pl.program_id(0)
    k = pl.program_id(1)
    nk = pl.num_programs(1)

    def step(x_ref, o_ref, h_ref):
        @pl.when(k == 0)
        def _():
            x = x_ref[...]
            h_ref[...] = _rms(x, g_ref[...]).astype(BF16)
            o_ref[...] = x

        a = jnp.maximum(_dot(h_ref[...], w1_ref[...]), 0.0)
        o_ref[...] += _dot((a * a).astype(BF16), w2_ref[...])

        if final:
            @pl.when(k == nk - 1)
            def _():
                o_ref[...] = _rms(o_ref[...], gf_ref[...])

    step(xp_ref, op_ref, hp_ref)

    @pl.when(i == NT - 1)
    def _():
        step(xs_ref, os_ref, hs_ref)


def _mlp(xp, xs, gain, w1, w2, gain_final, layer, final, fc=512):
    return pl.pallas_call(
        functools.partial(_mlp_kernel, final=final),
        grid=(NT, D_FF // fc),
        in_specs=[
            pl.BlockSpec((TM, D_MODEL), lambda i, k: (i, 0), pipeline_mode=pl.Buffered(1)),
            pl.BlockSpec((NS_ROWS, D_MODEL), lambda i, k: (0, 0)),
            pl.BlockSpec((1, D_MODEL), lambda i, k: (0, 0)),
            pl.BlockSpec((None, D_MODEL, fc), lambda i, k: (layer, 0, k)),
            pl.BlockSpec((None, fc, D_MODEL), lambda i, k: (layer, k, 0)),
            pl.BlockSpec((1, D_MODEL), lambda i, k: (0, 0)),
        ],
        out_specs=[
            pl.BlockSpec((TM, D_MODEL), lambda i, k: (i, 0)),
            pl.BlockSpec((NS_ROWS, D_MODEL), lambda i, k: (0, 0)),
        ],
        out_shape=[
            jax.ShapeDtypeStruct((NP_ROWS, D_MODEL), F32),
            jax.ShapeDtypeStruct((NS_ROWS, D_MODEL), F32),
        ],
        scratch_shapes=[pltpu.VMEM((TM, D_MODEL), BF16), pltpu.VMEM((NS_ROWS, D_MODEL), BF16)],
        compiler_params=_params(("arbitrary", "arbitrary")),
        name="mlp",
    )(xp, xs, gain, w1, w2, gain_final)


QKV_TN = 1024


def _qkv_kernel(xp_ref, xs_ref, g_ref, w_ref, op_ref, os_ref, hp_ref):
    i = pl.program_id(0)
    n = pl.program_id(1)

    @pl.when(n == 0)
    def _():
        hp_ref[...] = _rms(xp_ref[...], g_ref[...]).astype(BF16)

    res = _dot(hp_ref[...], w_ref[...])
    for c in range(QKV_TN // HEAD_DIM):
        op_ref[c] = res[:, c * HEAD_DIM:(c + 1) * HEAD_DIM]

    @pl.when(i == NT - 1)
    def _():
        os_ref[...] = _dot(_rms(xs_ref[...], g_ref[...]).astype(BF16), w_ref[...])


def _qkv(xp, xs, gain, w_qkv, layer):
    hb = QKV_TN // HEAD_DIM
    return pl.pallas_call(
        _qkv_kernel,
        grid=(NT, QKV_WIDTH // QKV_TN),
        in_specs=[
            pl.BlockSpec((TM, D_MODEL), lambda i, n: (i, 0)),
            pl.BlockSpec((NS_ROWS, D_MODEL), lambda i, n: (0, 0)),
            pl.BlockSpec((1, D_MODEL), lambda i, n: (0, 0)),
            pl.BlockSpec((None, D_MODEL, QKV_TN), lambda i, n: (layer, 0, n)),
        ],
        out_specs=[
            pl.BlockSpec((hb, TM, HEAD_DIM), lambda i, n: (n, i, 0)),
            pl.BlockSpec((NS_ROWS, QKV_TN), lambda i, n: (0, _sample_col(i, n))),
        ],
        out_shape=[
            jax.ShapeDtypeStruct((N_QKV_HEADS, NP_ROWS, HEAD_DIM), F32),
            jax.ShapeDtypeStruct((NS_ROWS, QKV_WIDTH), F32),
        ],
        scratch_shapes=[pltpu.VMEM((TM, D_MODEL), BF16)],
        compiler_params=_params(("arbitrary", "arbitrary")),
        name="qkv",
    )(xp, xs, gain, w_qkv)


MERGE_ROWS = 256


def _softmax_pv(s, v):
    m = jnp.max(s, axis=-1, keepdims=True)
    p = jnp.exp(s - m)
    l = jnp.sum(p, axis=-1, keepdims=True)
    o = _dot(p.astype(BF16), v) / l
    return o, m + jnp.log(l)


def _attn_prompt_kernel(q0, k0, v0, q1, k1, v1, q2, k2, v2, o_ref, og0, og1, og2, lg0, lg1, lg2):
    groups = ((q0, k0, v0, og0, lg0), (q1, k1, v1, og1, lg1), (q2, k2, v2, og2, lg2))
    nt_dims = (((1,), (1,)), ((), ()))

    qi = lax.broadcasted_iota(jnp.int32, (BAND, 2 * BAND), 0)
    ki = lax.broadcasted_iota(jnp.int32, (BAND, 2 * BAND), 1)
    band_mask = (ki >= qi) & (ki <= qi + BAND)
    qi1 = lax.broadcasted_iota(jnp.int32, (BAND, BAND), 0)
    ki1 = lax.broadcasted_iota(jnp.int32, (BAND, BAND), 1)
    causal_mask = ki1 <= qi1

    for g in range(N_GROUPS):
        d = DILATIONS[g]
        n_blocks = SEQ // (d * BAND)
        q_ref, k_ref, v_ref, og_ref, lg_ref = groups[g]

        def run_block(start, kstart, nkeys, mask, q_ref=q_ref, k_ref=k_ref, v_ref=v_ref,
                      og_ref=og_ref, lg_ref=lg_ref, d=d):
            qrows = pl.ds(start, BAND, stride=d) if d > 1 else pl.ds(start, BAND)
            krows = pl.ds(kstart, nkeys, stride=d) if d > 1 else pl.ds(kstart, nkeys)
            q = (q_ref[qrows, :] * SCALE).astype(BF16)
            k = k_ref[krows, :].astype(BF16)
            v = v_ref[krows, :].astype(BF16)
            s = lax.dot_general(q, k, nt_dims, preferred_element_type=F32)
            o, lse = _softmax_pv(jnp.where(mask, s, NEG_INF), v)
            og_ref[qrows, :] = o
            lg_ref[qrows, :] = jnp.broadcast_to(lse, (BAND, HEAD_DIM))

        def residue(r, carry, run_block=run_block, d=d, n_blocks=n_blocks):
            run_block(r, r, BAND, causal_mask)

            def later_block(jb, c):
                start = r + jb * (BAND * d)
                run_block(start, start - BAND * d, 2 * BAND, band_mask)
                return c

            if n_blocks > 1:
                lax.fori_loop(1, n_blocks, later_block, 0)
            return carry

        if d > 1:
            lax.fori_loop(0, d, residue, 0)
        else:
            residue(0, 0)

    def merge(c, carry):
        rows = pl.ds(pl.multiple_of(c * MERGE_ROWS, MERGE_ROWS), MERGE_ROWS)
        l0, l1, l2 = lg0[rows, :], lg1[rows, :], lg2[rows, :]
        m = jnp.maximum(jnp.maximum(l0, l1), l2)
        w0, w1, w2 = jnp.exp(l0 - m), jnp.exp(l1 - m), jnp.exp(l2 - m)
        num = w0 * og0[rows, :] + w1 * og1[rows, :] + w2 * og2[rows, :]
        o_ref[rows, :] = num / (w0 + w1 + w2)
        return carry

    lax.fori_loop(0, SEQ // MERGE_ROWS, merge, 0)


def _attn_prompt(qkv_t):
    def col(c):
        return pl.BlockSpec((None, SEQ, HEAD_DIM), lambda b, h: (c * HEADS + h, b, 0))

    seq_buf = pltpu.VMEM((SEQ, HEAD_DIM), F32)
    return pl.pallas_call(
        _attn_prompt_kernel,
        grid=(BATCH, HEADS),
        in_specs=[col(c) for c in range(3 * N_GROUPS)],
        out_specs=pl.BlockSpec((SEQ, HEAD_DIM), lambda b, h: (b, h)),
        out_shape=jax.ShapeDtypeStruct((NP_ROWS, GROUP_WIDTH), F32),
        scratch_shapes=[seq_buf] * 6,
        compiler_params=_params(("arbitrary", "arbitrary")),
        name="attn_prompt",
    )(*([qkv_t] * (3 * N_GROUPS)))


def _attn_sample_kernel(qs_ref, c0_ref, c1_ref, c2_ref, o_ref):
    caches = (c0_ref, c1_ref, c2_ref)
    for j in range(DEC_SEQ):
        outs, lses = [], []
        for g in range(N_GROUPS):
            d = DILATIONS[g]
            base = g * 3 * HEADS
            q = qs_ref[j, base:base + HEADS, :] * SCALE
            a = j // d
            res = j % d
            kc = caches[g][a:BAND, res, 0]
            vc = caches[g][a:BAND, res, 1]
            sc = jnp.sum(kc * q[None], axis=-1, keepdims=True)
            new_rows = [j - d * n for n in range(a + 1)]
            kn = [qs_ref[jj, base + HEADS:base + 2 * HEADS, :] for jj in new_rows]
            vn = [qs_ref[jj, base + 2 * HEADS:base + 3 * HEADS, :] for jj in new_rows]
            sn = [jnp.sum(kk * q, axis=-1, keepdims=True) for kk in kn]
            m = jnp.max(sc, axis=0)
            for s_ in sn:
                m = jnp.maximum(m, s_)
            pc = jnp.exp(sc - m[None])
            l = jnp.sum(pc, axis=0)
            acc = jnp.sum(pc * vc, axis=0)
            for s_, vv in zip(sn, vn):
                p_ = jnp.exp(s_ - m)
                l = l + p_
                acc = acc + p_ * vv
            outs.append(acc / l)
            lses.append(m + jnp.log(l))
        m = jnp.maximum(jnp.maximum(lses[0], lses[1]), lses[2])
        w = [jnp.exp(ls - m) for ls in lses]
        o_ref[j] = (w[0] * outs[0] + w[1] * outs[1] + w[2] * outs[2]) / (w[0] + w[1] + w[2])


def _attn_sample(qkv_s, cache0, cache1, cache2, layer):
    assert all(c.shape[2] == w for c, w in zip((cache0, cache1, cache2), WINDOWS))
    qs = qkv_s.reshape(DEC_BATCH, DEC_SEQ, N_QKV_HEADS, HEAD_DIM)
    views = [c.reshape(c.shape[0], DEC_BATCH, BAND, d, 2, HEADS, HEAD_DIM)
             for c, d in zip((cache0, cache1, cache2), DILATIONS)]

    def cache_spec(d):
        return pl.BlockSpec((None, None, BAND, min(d, DEC_SEQ), 2, HEADS, HEAD_DIM),
                            lambda b: (layer, b, 0, 0, 0, 0, 0))

    return pl.pallas_call(
        _attn_sample_kernel,
        grid=(DEC_BATCH,),
        in_specs=[pl.BlockSpec((None, DEC_SEQ, N_QKV_HEADS, HEAD_DIM), lambda b: (b, 0, 0, 0))]
        + [cache_spec(d) for d in DILATIONS],
        out_specs=pl.BlockSpec((None, DEC_SEQ, HEADS, HEAD_DIM), lambda b: (b, 0, 0, 0)),
        out_shape=jax.ShapeDtypeStruct((DEC_BATCH, DEC_SEQ, HEADS, HEAD_DIM), F32),
        compiler_params=_params(("arbitrary",)),
        name="attn_sample",
    )(qs, *views).reshape(NS_ROWS, GROUP_WIDTH)


def _proj_kernel(ap_ref, as_ref, xp_ref, xs_ref, w_ref, op_ref, os_ref):
    op_ref[...] = xp_ref[...] + _dot(ap_ref[...].astype(BF16), w_ref[...])

    @pl.when(pl.program_id(0) == NT - 1)
    def _():
        os_ref[...] = xs_ref[...] + _dot(as_ref[...].astype(BF16), w_ref[...])


def _proj(ap, as_, xp, xs, w_o, layer):
    return pl.pallas_call(
        _proj_kernel,
        grid=(NT,),
        in_specs=[
            pl.BlockSpec((TM, GROUP_WIDTH), lambda i: (i, 0)),
            pl.BlockSpec((NS_ROWS, GROUP_WIDTH), lambda i: (0, 0)),
            pl.BlockSpec((TM, D_MODEL), lambda i: (i, 0)),
            pl.BlockSpec((NS_ROWS, D_MODEL), lambda i: (0, 0)),
            pl.BlockSpec((None, GROUP_WIDTH, D_MODEL), lambda i: (layer, 0, 0)),
        ],
        out_specs=[
            pl.BlockSpec((TM, D_MODEL), lambda i: (i, 0)),
            pl.BlockSpec((NS_ROWS, D_MODEL), lambda i: (0, 0)),
        ],
        out_shape=[
            jax.ShapeDtypeStruct((NP_ROWS, D_MODEL), F32),
            jax.ShapeDtypeStruct((NS_ROWS, D_MODEL), F32),
        ],
        compiler_params=_params(("arbitrary",)),
        name="attn_proj",
    )(ap, as_, xp, xs, w_o)


def _prompt_kv(qkv_t, g):
    keep = min(WINDOWS[g], SEQ)
    kv = qkv_t[(3 * g + 1) * HEADS:(3 * g + 3) * HEADS]
    kv = kv.reshape(2, HEADS, BATCH, SEQ, HEAD_DIM)[:, :, :, SEQ - keep:]
    return jnp.transpose(kv, (2, 3, 0, 1, 4))


def _sample_kv(qkv_s, g):
    kv = qkv_s.reshape(DEC_BATCH, DEC_SEQ, N_GROUPS, 3, HEADS, HEAD_DIM)
    return kv[:, :, g, 1:3]


def kernel(x_prompt, x_sample, state_conv, cache_kv_w128, cache_kv_w512, cache_kv_w2048, norm_mix, norm_mlp, norm_final, conv_w_pw1, conv_b_pw1, conv_w_dw, conv_b_dw, conv_ln_g, conv_ln_b, conv_w_pw2, conv_b_pw2, attn_w_qkv, attn_w_o, mlp_w1, mlp_w2):
    xp = x_prompt.reshape(NP_ROWS, D_MODEL)
    xs = x_sample.reshape(NS_ROWS, D_MODEL)

    w_pw1 = conv_w_pw1.astype(BF16)
    w_pw2 = conv_w_pw2.astype(BF16)
    w_qkv = attn_w_qkv.astype(BF16)
    w_o = attn_w_o.astype(BF16)
    w1 = mlp_w1.astype(BF16)
    w2 = mlp_w2.astype(BF16)
    b_pw1 = conv_b_pw1[:, None, :]
    b_dw = conv_b_dw[:, None, :]
    ln_g = conv_ln_g[:, None, :]
    ln_b = conv_ln_b[:, None, :]
    b_pw2 = conv_b_pw2[:, None, :]
    g_final = norm_final[None, :]

    conv_p, conv_s = [], []
    kv_p = [[] for _ in range(N_GROUPS)]
    kv_s = [[] for _ in range(N_GROUPS)]
    for i in range(DEPTH):
        j = i // 2
        g_mix = norm_mix[i][None, :]
        g_mlp = norm_mlp[i][None, :]
        if i % 2 == 0:
            glu_p, glu_s = _pw1_glu(xp, xs, g_mix, w_pw1, b_pw1, j)
            xp_new = _conv_prompt(glu_p, xp, conv_w_dw, b_dw, ln_g, ln_b, w_pw2, b_pw2, j)
            xs, new_state = _conv_sample(glu_s, state_conv, xs, conv_w_dw, b_dw, ln_g, ln_b, w_pw2, b_pw2, j)
            xp = xp_new
            conv_p.append(glu_p.reshape(BATCH, SEQ, D_MODEL)[:, SEQ - CONV_STATE:])
            conv_s.append(new_state)
        else:
            qkv_t, qkv_s = _qkv(xp, xs, g_mix, w_qkv, j)
            attn_p = _attn_prompt(qkv_t)
            attn_s = _attn_sample(qkv_s, cache_kv_w128, cache_kv_w512, cache_kv_w2048, j)
            xp, xs = _proj(attn_p, attn_s, xp, xs, w_o, j)
            for g in range(N_GROUPS):
                kv_p[g].append(_prompt_kv(qkv_t, g))
                kv_s[g].append(_sample_kv(qkv_s, g))
        xp, xs = _mlp(xp, xs, g_mlp, w1, w2, g_final, i, final=(i == DEPTH - 1))

    return (xp.reshape(BATCH, SEQ, D_MODEL), xs.reshape(DEC_BATCH, DEC_SEQ, D_MODEL),
            jnp.stack(conv_p), jnp.stack(conv_s),
            jnp.stack(kv_p[0]), jnp.stack(kv_s[0]),
            jnp.stack(kv_p[1]), jnp.stack(kv_s[1]),
            jnp.stack(kv_p[2]), jnp.stack(kv_s[2]))
```

```python
import functools

import jax
import jax.numpy as jnp
from jax import lax
from jax.experimental import pallas as pl
from jax.experimental.pallas import tpu as pltpu

D_MODEL = 2048
BATCH = 4
SEQ = 2048
DEPTH = 4
DEC_BATCH = 8
DEC_SEQ = 8
CONV_WIDTH = 31
CONV_STATE = CONV_WIDTH - 1
WINDOWS = (128, 512, 2048)
DILATIONS = (1, 4, 16)
N_GROUPS = 3
HEAD_DIM = 128
HEADS = 8
GROUP_WIDTH = HEADS * HEAD_DIM
QKV_WIDTH = N_GROUPS * 3 * GROUP_WIDTH
N_QKV_HEADS = QKV_WIDTH // HEAD_DIM
D_FF = 4 * D_MODEL
RMS_EPS = 1e-6
LN_EPS = 1e-5
NEG_INF = -1e30
SCALE = HEAD_DIM ** -0.5
BAND = 128

NP_ROWS = BATCH * SEQ
NS_ROWS = DEC_BATCH * DEC_SEQ
TM = 1024
NT = NP_ROWS // TM
TM_CONV = 512
HALO = 32
V7X_VMEM_LIMIT = 56 * 1024 * 1024

BF16 = jnp.bfloat16
F32 = jnp.float32


def _params(sem):
    return pltpu.CompilerParams(dimension_semantics=sem, vmem_limit_bytes=V7X_VMEM_LIMIT)


def _rms(x, g):
    ms = jnp.mean(x * x, axis=-1, keepdims=True)
    return x * lax.rsqrt(ms + RMS_EPS) * g


def _dot(a, b):
    return jnp.dot(a, b, preferred_element_type=F32)


def _sample_col(i, n):
    return jnp.where(i == NT - 1, n, 0)


def _pw1_glu_kernel(xp_ref, xs_ref, g_ref, wa_ref, wg_ref, ba_ref, bg_ref, op_ref, os_ref, hp_ref):
    i = pl.program_id(0)
    n = pl.program_id(1)

    @pl.when(n == 0)
    def _():
        hp_ref[...] = _rms(xp_ref[...], g_ref[...]).astype(BF16)

    wa = wa_ref[...].astype(BF16)
    wg = wg_ref[...].astype(BF16)

    def glu(h):
        a = _dot(h, wa) + ba_ref[...]
        gate = _dot(h, wg) + bg_ref[...]
        return a * jax.nn.sigmoid(gate)

    op_ref[...] = glu(hp_ref[...])

    @pl.when(i == NT - 1)
    def _():
        os_ref[...] = glu(_rms(xs_ref[...], g_ref[...]).astype(BF16))


def _pw1_glu(xp, xs, gain, w_pw1, b_pw1, layer, tn=512):
    nn = D_MODEL // tn
    return pl.pallas_call(
        _pw1_glu_kernel,
        grid=(NT, nn),
        in_specs=[
            pl.BlockSpec((TM, D_MODEL), lambda i, n: (i, 0)),
            pl.BlockSpec((NS_ROWS, D_MODEL), lambda i, n: (0, 0)),
            pl.BlockSpec((1, D_MODEL), lambda i, n: (0, 0)),
            pl.BlockSpec((None, D_MODEL, tn), lambda i, n: (layer, 0, n)),
            pl.BlockSpec((None, D_MODEL, tn), lambda i, n: (layer, 0, n + nn)),
            pl.BlockSpec((None, 1, tn), lambda i, n: (layer, 0, n)),
            pl.BlockSpec((None, 1, tn), lambda i, n: (layer, 0, n + nn)),
        ],
        out_specs=[
            pl.BlockSpec((TM, tn), lambda i, n: (i, n)),
            pl.BlockSpec((NS_ROWS, tn), lambda i, n: (0, _sample_col(i, n))),
        ],
        out_shape=[
            jax.ShapeDtypeStruct((NP_ROWS, D_MODEL), F32),
            jax.ShapeDtypeStruct((NS_ROWS, D_MODEL), F32),
        ],
        scratch_shapes=[pltpu.VMEM((TM, D_MODEL), BF16)],
        compiler_params=_params(("arbitrary", "arbitrary")),
        name="pw1_glu",
    )(xp, xs, gain, w_pw1, w_pw1, b_pw1, b_pw1)


CONV_LANES = 256
CONV_ROWS = 128
LN_ROWS = 64


def _depthwise_rows(window, w_ref, lanes):
    rows = CONV_ROWS + 8
    acc = None
    for r in range(8):
        part = None
        for q in range(4):
            s = 8 * q + r
            if s > CONV_STATE:
                continue
            tap = w_ref[pl.ds(CONV_STATE - s, 1), lanes]
            term = tap * window[HALO - 8 - 8 * q:HALO - 8 - 8 * q + rows, :]
            part = term if part is None else part + term
        if r:
            part = pltpu.roll(part, r, 0)
        acc = part if acc is None else acc + part
    return acc[8:, :]


def _ln_swish(c, g, b):
    mu = jnp.mean(c, axis=-1, keepdims=True)
    cc = c - mu
    var = jnp.mean(cc * cc, axis=-1, keepdims=True)
    y = cc * lax.rsqrt(var + LN_EPS) * g + b
    return y * jax.nn.sigmoid(y)


def _conv_prompt_kernel(glu_ref, prev_ref, x_ref, wdw_ref, bdw_ref, lng_ref, lnb_ref, w2_ref, b2_ref,
                        o_ref, c_ref, s_ref):
    i = pl.program_id(0)
    seq_start = (i % (SEQ // TM_CONV)) == 0

    def lane_chunk(c, carry):
        lanes = pl.ds(pl.multiple_of(c * CONV_LANES, CONV_LANES), CONV_LANES)
        hist = jnp.where(seq_start, 0.0, prev_ref[:, lanes])
        bias = bdw_ref[:, lanes]
        for rc in range(TM_CONV // CONV_ROWS):
            t0 = rc * CONV_ROWS
            if rc == 0:
                window = jnp.concatenate([hist, glu_ref[0:CONV_ROWS, lanes]], axis=0)
            else:
                window = glu_ref[t0 - HALO:t0 + CONV_ROWS, lanes]
            c_ref[t0:t0 + CONV_ROWS, lanes] = _depthwise_rows(window, wdw_ref, lanes) + bias
        return carry

    lax.fori_loop(0, D_MODEL // CONV_LANES, lane_chunk, 0)

    def ln_chunk(rc, carry):
        rows = pl.ds(pl.multiple_of(rc * LN_ROWS, LN_ROWS), LN_ROWS)
        s_ref[rows, :] = _ln_swish(c_ref[rows, :], lng_ref[...], lnb_ref[...]).astype(BF16)
        return carry

    lax.fori_loop(0, TM_CONV // LN_ROWS, ln_chunk, 0)
    o_ref[...] = x_ref[...] + _dot(s_ref[...], w2_ref[...]) + b2_ref[...]


def _conv_prompt(glu_p, xp, w_dw, b_dw, ln_g, ln_b, w_pw2, b_pw2, layer):
    halo_blocks = TM_CONV // HALO
    vec = pl.BlockSpec((None, 1, D_MODEL), lambda i: (layer, 0, 0))
    return pl.pallas_call(
        _conv_prompt_kernel,
        grid=(NP_ROWS // TM_CONV,),
        in_specs=[
            pl.BlockSpec((TM_CONV, D_MODEL), lambda i: (i, 0)),
            pl.BlockSpec((HALO, D_MODEL), lambda i: (jnp.maximum(i * halo_blocks - 1, 0), 0)),
            pl.BlockSpec((TM_CONV, D_MODEL), lambda i: (i, 0)),
            pl.BlockSpec((None, CONV_WIDTH, D_MODEL), lambda i: (layer, 0, 0)),
            vec, vec, vec,
            pl.BlockSpec((None, D_MODEL, D_MODEL), lambda i: (layer, 0, 0),
                         pipeline_mode=pl.Buffered(1)),
            vec,
        ],
        out_specs=pl.BlockSpec((TM_CONV, D_MODEL), lambda i: (i, 0)),
        out_shape=jax.ShapeDtypeStruct((NP_ROWS, D_MODEL), F32),
        scratch_shapes=[pltpu.VMEM((TM_CONV, D_MODEL), F32), pltpu.VMEM((TM_CONV, D_MODEL), BF16)],
        compiler_params=_params(("arbitrary",)),
        name="conv_prompt",
    )(glu_p, glu_p, xp, w_dw, b_dw, ln_g, ln_b, w_pw2, b_pw2)


PAD_ROWS = 40


def _conv_sample_kernel(glu_ref, st_ref, x_ref, wdw_ref, bdw_ref, lng_ref, lnb_ref, w2_ref, b2_ref,
                        o_ref, nst_ref, pad_ref, c_ref):
    for b in range(DEC_BATCH):
        pad_ref[0:CONV_STATE, :] = st_ref[b]
        pad_ref[CONV_STATE:CONV_STATE + DEC_SEQ, :] = glu_ref[b * DEC_SEQ:(b + 1) * DEC_SEQ, :]
        acc = None
        for k in range(CONV_WIDTH):
            term = wdw_ref[pl.ds(k, 1), :] * pad_ref[pl.ds(k, DEC_SEQ), :]
            acc = term if acc is None else acc + term
        c_ref[b * DEC_SEQ:(b + 1) * DEC_SEQ, :] = acc + bdw_ref[...]
        nst_ref[b] = pad_ref[DEC_SEQ:DEC_SEQ + CONV_STATE, :]
    s = _ln_swish(c_ref[...], lng_ref[...], lnb_ref[...]).astype(BF16)
    o_ref[...] = x_ref[...] + _dot(s, w2_ref[...]) + b2_ref[...]


def _conv_sample(glu_s, state, xs, w_dw, b_dw, ln_g, ln_b, w_pw2, b_pw2, layer):
    vec = pl.BlockSpec((None, 1, D_MODEL), lambda i: (layer, 0, 0))
    rows = pl.BlockSpec((NS_ROWS, D_MODEL), lambda i: (0, 0))
    st = pl.BlockSpec((None, DEC_BATCH, CONV_STATE, D_MODEL), lambda i: (layer, 0, 0, 0))
    return pl.pallas_call(
        _conv_sample_kernel,
        grid=(1,),
        in_specs=[
            rows, st, rows,
            pl.BlockSpec((None, CONV_WIDTH, D_MODEL), lambda i: (layer, 0, 0)),
            vec, vec, vec,
            pl.BlockSpec((None, D_MODEL, D_MODEL), lambda i: (layer, 0, 0)),
            vec,
        ],
        out_specs=[rows, pl.BlockSpec((DEC_BATCH, CONV_STATE, D_MODEL), lambda i: (0, 0, 0))],
        out_shape=[
            jax.ShapeDtypeStruct((NS_ROWS, D_MODEL), F32),
            jax.ShapeDtypeStruct((DEC_BATCH, CONV_STATE, D_MODEL), F32),
        ],
        scratch_shapes=[pltpu.VMEM((PAD_ROWS, D_MODEL), F32), pltpu.VMEM((NS_ROWS, D_MODEL), F32)],
        compiler_params=_params(("arbitrary",)),
        name="conv_sample",
    )(glu_s, state, xs, w_dw, b_dw, ln_g, ln_b, w_pw2, b_pw2)


def _mlp_kernel(xp_ref, xs_ref, g_ref, w1_ref, w2_ref, gf_ref, op_ref, os_ref, hp_ref, hs_ref, *, final):
    i = pl.program_id(0)
    k = pl.program_id(1)
    nk = pl.num_programs(1)

    w1 = w1_ref[...].astype(BF16)
    w2 = w2_ref[...].astype(BF16)

    def step(x_ref, o_ref, h_ref):
        @pl.when(k == 0)
        def _():
            x = x_ref[...]
            h_ref[...] = _rms(x, g_ref[...]).astype(BF16)
            o_ref[...] = x

        a = jnp.maximum(_dot(h_ref[...], w1), 0.0)
        o_ref[...] += _dot((a * a).astype(BF16), w2)

        if final:
            @pl.when(k == nk - 1)
            def _():
                o_ref[...] = _rms(o_ref[...], gf_ref[...])

    step(xp_ref, op_ref, hp_ref)

    @pl.when(i == NT - 1)
    def _():
        step(xs_ref, os_ref, hs_ref)


def _mlp(xp, xs, gain, w1, w2, gain_final, layer, final, fc=512):
    return pl.pallas_call(
        functools.partial(_mlp_kernel, final=final),
        grid=(NT, D_FF // fc),
        in_specs=[
            pl.BlockSpec((TM, D_MODEL), lambda i, k: (i, 0), pipeline_mode=pl.Buffered(1)),
            pl.BlockSpec((NS_ROWS, D_MODEL), lambda i, k: (0, 0)),
            pl.BlockSpec((1, D_MODEL), lambda i, k: (0, 0)),
            pl.BlockSpec((None, D_MODEL, fc), lambda i, k: (layer, 0, k)),
            pl.BlockSpec((None, fc, D_MODEL), lambda i, k: (layer, k, 0)),
            pl.BlockSpec((1, D_MODEL), lambda i, k: (0, 0)),
        ],
        out_specs=[
            pl.BlockSpec((TM, D_MODEL), lambda i, k: (i, 0)),
            pl.BlockSpec((NS_ROWS, D_MODEL), lambda i, k: (0, 0)),
        ],
        out_shape=[
            jax.ShapeDtypeStruct((NP_ROWS, D_MODEL), F32),
            jax.ShapeDtypeStruct((NS_ROWS, D_MODEL), F32),
        ],
        scratch_shapes=[pltpu.VMEM((TM, D_MODEL), BF16), pltpu.VMEM((NS_ROWS, D_MODEL), BF16)],
        compiler_params=_params(("arbitrary", "arbitrary")),
        name="mlp",
    )(xp, xs, gain, w1, w2, gain_final)


QKV_TN = GROUP_WIDTH
TILES_PER_SEQ = SEQ // TM
KEEP = tuple(min(w, SEQ) for w in WINDOWS)
KEEP_ROWS = tuple(min(k, TM) for k in KEEP)
KEEP_TILES = tuple(k // r for k, r in zip(KEEP, KEEP_ROWS))


def _kv_writer(i, n, g):
    in_kept_tile = (i % TILES_PER_SEQ) >= TILES_PER_SEQ - KEEP_TILES[g]
    return in_kept_tile & ((n == 3 * g + 1) | (n == 3 * g + 2))


def _kv_block_index(i, n, g):
    skipped = TILES_PER_SEQ - KEEP_TILES[g]
    b = i // TILES_PER_SEQ
    t = i % TILES_PER_SEQ - skipped
    kept = t >= 0
    first = i < skipped
    which = (n >= 3 * g + 2).astype(jnp.int32)
    b = jnp.where(kept, b, jnp.where(first, 0, b - 1))
    t = jnp.where(kept, t, jnp.where(first, 0, KEEP_TILES[g] - 1))
    which = jnp.where(kept, which, jnp.where(first, 0, 1))
    return b, t, which, 0, 0


def _qkv_kernel(xp_ref, xs_ref, g_ref, w_ref, op_ref, os_ref, kv0_ref, kv1_ref, kv2_ref, hp_ref):
    i = pl.program_id(0)
    n = pl.program_id(1)

    @pl.when(n == 0)
    def _():
        hp_ref[...] = _rms(xp_ref[...], g_ref[...]).astype(BF16)

    res = _dot(hp_ref[...], w_ref[...])
    for c in range(QKV_TN // HEAD_DIM):
        op_ref[c] = res[:, c * HEAD_DIM:(c + 1) * HEAD_DIM]

    for g, kv_ref in enumerate((kv0_ref, kv1_ref, kv2_ref)):
        @pl.when(_kv_writer(i, n, g))
        def _(kv_ref=kv_ref, g=g):
            kv_ref[...] = res[TM - KEEP_ROWS[g]:, :].reshape(KEEP_ROWS[g], HEADS, HEAD_DIM)

    @pl.when(i == NT - 1)
    def _():
        os_ref[...] = _dot(_rms(xs_ref[...], g_ref[...]).astype(BF16), w_ref[...])


def _qkv(xp, xs, gain, w_qkv, layer):
    hb = QKV_TN // HEAD_DIM

    def kv_spec(g):
        return pl.BlockSpec((None, KEEP_ROWS[g], None, HEADS, HEAD_DIM),
                            lambda i, n: _kv_block_index(i, n, g))

    return pl.pallas_call(
        _qkv_kernel,
        grid=(NT, QKV_WIDTH // QKV_TN),
        in_specs=[
            pl.BlockSpec((TM, D_MODEL), lambda i, n: (i, 0)),
            pl.BlockSpec((NS_ROWS, D_MODEL), lambda i, n: (0, 0)),
            pl.BlockSpec((1, D_MODEL), lambda i, n: (0, 0)),
            pl.BlockSpec((None, D_MODEL, QKV_TN), lambda i, n: (layer, 0, n)),
        ],
        out_specs=[
            pl.BlockSpec((hb, TM, HEAD_DIM), lambda i, n: (n, i, 0)),
            pl.BlockSpec((NS_ROWS, QKV_TN), lambda i, n: (0, _sample_col(i, n))),
        ] + [kv_spec(g) for g in range(N_GROUPS)],
        out_shape=[
            jax.ShapeDtypeStruct((N_QKV_HEADS, NP_ROWS, HEAD_DIM), F32),
            jax.ShapeDtypeStruct((NS_ROWS, QKV_WIDTH), F32),
        ] + [jax.ShapeDtypeStruct((BATCH, KEEP[g], 2, HEADS, HEAD_DIM), F32) for g in range(N_GROUPS)],
        scratch_shapes=[pltpu.VMEM((TM, D_MODEL), BF16)],
        compiler_params=_params(("arbitrary", "arbitrary")),
        name="qkv",
    )(xp, xs, gain, w_qkv)


MERGE_ROWS = 256


def _softmax_pv(s, v):
    m = jnp.max(s, axis=-1, keepdims=True)
    p = jnp.exp(s - m)
    l = jnp.sum(p, axis=-1, keepdims=True)
    o = _dot(p.astype(BF16), v) / l
    return o, m + jnp.log(l)


def _attn_prompt_kernel(q0, k0, v0, q1, k1, v1, q2, k2, v2, o_ref, og0, og1, og2, lg0, lg1, lg2):
    groups = ((q0, k0, v0, og0, lg0), (q1, k1, v1, og1, lg1), (q2, k2, v2, og2, lg2))
    nt_dims = (((1,), (1,)), ((), ()))

    def iota(shape, dim):
        return lax.broadcasted_iota(jnp.int32, shape, dim)

    qi, ki = iota((BAND, 2 * BAND), 0), iota((BAND, 2 * BAND), 1)
    band_mask = (ki >= qi) & (ki <= qi + BAND)
    qi2, ki2 = iota((2 * BAND, 2 * BAND), 0), iota((2 * BAND, 2 * BAND), 1)
    head_mask = (ki2 <= qi2) & (ki2 >= qi2 - BAND)
    pair_mask = (ki2 <= qi2) & ((ki2 >= BAND) == (qi2 >= BAND))

    def attend(refs, q_slices, k_slices, mask):
        q_ref, k_ref, v_ref, og_ref, lg_ref = refs

        def load(ref, slices):
            parts = [ref[sl, :] for sl in slices]
            return parts[0] if len(parts) == 1 else jnp.concatenate(parts, axis=0)

        q = (load(q_ref, q_slices) * SCALE).astype(BF16)
        k = load(k_ref, k_slices).astype(BF16)
        v = load(v_ref, k_slices).astype(BF16)
        s = lax.dot_general(q, k, nt_dims, preferred_element_type=F32)
        o, lse = _softmax_pv(jnp.where(mask, s, NEG_INF), v)
        lse = jnp.broadcast_to(lse, o.shape)
        row = 0
        for sl in q_slices:
            og_ref[sl, :] = o[row:row + sl.size]
            lg_ref[sl, :] = lse[row:row + sl.size]
            row += sl.size

    for g in range(N_GROUPS):
        d = DILATIONS[g]
        n_blocks = SEQ // (d * BAND)

        def rows(r, first, count, d=d):
            start = r + first * BAND * d
            return pl.ds(start, count * BAND, stride=d) if d > 1 else pl.ds(start, count * BAND)

        if n_blocks == 1:
            assert d % 2 == 0
            for r in range(0, d, 2):
                pair = [rows(r, 0, 1), rows(r + 1, 0, 1)]
                attend(groups[g], pair, pair, pair_mask)
        else:
            for r in range(d):
                attend(groups[g], [rows(r, 0, 2)], [rows(r, 0, 2)], head_mask)
                for jb in range(2, n_blocks):
                    attend(groups[g], [rows(r, jb, 1)], [rows(r, jb - 1, 2)], band_mask)

    def merge(c, carry):
        rows = pl.ds(pl.multiple_of(c * MERGE_ROWS, MERGE_ROWS), MERGE_ROWS)
        l0, l1, l2 = lg0[rows, :], lg1[rows, :], lg2[rows, :]
        m = jnp.maximum(jnp.maximum(l0, l1), l2)
        w0, w1, w2 = jnp.exp(l0 - m), jnp.exp(l1 - m), jnp.exp(l2 - m)
        num = w0 * og0[rows, :] + w1 * og1[rows, :] + w2 * og2[rows, :]
        o_ref[rows, :] = num / (w0 + w1 + w2)
        return carry

    lax.fori_loop(0, SEQ // MERGE_ROWS, merge, 0)


def _attn_prompt(qkv_t):
    def col(c):
        return pl.BlockSpec((None, SEQ, HEAD_DIM), lambda b, h: (c * HEADS + h, b, 0))

    seq_buf = pltpu.VMEM((SEQ, HEAD_DIM), F32)
    return pl.pallas_call(
        _attn_prompt_kernel,
        grid=(BATCH, HEADS),
        in_specs=[col(c) for c in range(3 * N_GROUPS)],
        out_specs=pl.BlockSpec((SEQ, HEAD_DIM), lambda b, h: (b, h)),
        out_shape=jax.ShapeDtypeStruct((NP_ROWS, GROUP_WIDTH), F32),
        scratch_shapes=[seq_buf] * 6,
        compiler_params=_params(("arbitrary", "arbitrary")),
        name="attn_prompt",
    )(*([qkv_t] * (3 * N_GROUPS)))


def _attn_sample_kernel(qs_ref, c0_ref, c1_ref, c2_ref, o_ref):
    caches = (c0_ref, c1_ref, c2_ref)
    for j in range(DEC_SEQ):
        outs, lses = [], []
        for g in range(N_GROUPS):
            d = DILATIONS[g]
            base = g * 3 * HEADS
            q = qs_ref[j, base:base + HEADS, :] * SCALE
            a = j // d
            res = j % d
            kc = caches[g][a:BAND, res, 0]
            vc = caches[g][a:BAND, res, 1]
            sc = jnp.sum(kc * q[None], axis=-1, keepdims=True)
            new_rows = [j - d * n for n in range(a + 1)]
            kn = [qs_ref[jj, base + HEADS:base + 2 * HEADS, :] for jj in new_rows]
            vn = [qs_ref[jj, base + 2 * HEADS:base + 3 * HEADS, :] for jj in new_rows]
            sn = [jnp.sum(kk * q, axis=-1, keepdims=True) for kk in kn]
            m = jnp.max(sc, axis=0)
            for s_ in sn:
                m = jnp.maximum(m, s_)
            pc = jnp.exp(sc - m[None])
            l = jnp.sum(pc, axis=0)
            acc = jnp.sum(pc * vc, axis=0)
            for s_, vv in zip(sn, vn):
                p_ = jnp.exp(s_ - m)
                l = l + p_
                acc = acc + p_ * vv
            outs.append(acc / l)
            lses.append(m + jnp.log(l))
        m = jnp.maximum(jnp.maximum(lses[0], lses[1]), lses[2])
        w = [jnp.exp(ls - m) for ls in lses]
        o_ref[j] = (w[0] * outs[0] + w[1] * outs[1] + w[2] * outs[2]) / (w[0] + w[1] + w[2])


def _attn_sample(qkv_s, cache0, cache1, cache2, layer):
    assert all(c.shape[2] == w for c, w in zip((cache0, cache1, cache2), WINDOWS))
    qs = qkv_s.reshape(DEC_BATCH, DEC_SEQ, N_QKV_HEADS, HEAD_DIM)
    views = [c.reshape(c.shape[0], DEC_BATCH, BAND, d, 2, HEADS, HEAD_DIM)
             for c, d in zip((cache0, cache1, cache2), DILATIONS)]

    def cache_spec(d):
        return pl.BlockSpec((None, None, BAND, min(d, DEC_SEQ), 2, HEADS, HEAD_DIM),
                            lambda b: (layer, b, 0, 0, 0, 0, 0))

    return pl.pallas_call(
        _attn_sample_kernel,
        grid=(DEC_BATCH,),
        in_specs=[pl.BlockSpec((None, DEC_SEQ, N_QKV_HEADS, HEAD_DIM), lambda b: (b, 0, 0, 0))]
        + [cache_spec(d) for d in DILATIONS],
        out_specs=pl.BlockSpec((None, DEC_SEQ, HEADS, HEAD_DIM), lambda b: (b, 0, 0, 0)),
        out_shape=jax.ShapeDtypeStruct((DEC_BATCH, DEC_SEQ, HEADS, HEAD_DIM), F32),
        compiler_params=_params(("arbitrary",)),
        name="attn_sample",
    )(qs, *views).reshape(NS_ROWS, GROUP_WIDTH)


def _proj_kernel(ap_ref, as_ref, xp_ref, xs_ref, w_ref, op_ref, os_ref):
    op_ref[...] = xp_ref[...] + _dot(ap_ref[...].astype(BF16), w_ref[...])

    @pl.when(pl.program_id(0) == NT - 1)
    def _():
        os_ref[...] = xs_ref[...] + _dot(as_ref[...].astype(BF16), w_ref[...])


def _proj(ap, as_, xp, xs, w_o, layer):
    return pl.pallas_call(
        _proj_kernel,
        grid=(NT,),
        in_specs=[
            pl.BlockSpec((TM, GROUP_WIDTH), lambda i: (i, 0)),
            pl.BlockSpec((NS_ROWS, GROUP_WIDTH), lambda i: (0, 0)),
            pl.BlockSpec((TM, D_MODEL), lambda i: (i, 0)),
            pl.BlockSpec((NS_ROWS, D_MODEL), lambda i: (0, 0)),
            pl.BlockSpec((None, GROUP_WIDTH, D_MODEL), lambda i: (layer, 0, 0)),
        ],
        out_specs=[
            pl.BlockSpec((TM, D_MODEL), lambda i: (i, 0)),
            pl.BlockSpec((NS_ROWS, D_MODEL), lambda i: (0, 0)),
        ],
        out_shape=[
            jax.ShapeDtypeStruct((NP_ROWS, D_MODEL), F32),
            jax.ShapeDtypeStruct((NS_ROWS, D_MODEL), F32),
        ],
        compiler_params=_params(("arbitrary",)),
        name="attn_proj",
    )(ap, as_, xp, xs, w_o)


def _sample_kv(qkv_s, g):
    kv = qkv_s.reshape(DEC_BATCH, DEC_SEQ, N_GROUPS, 3, HEADS, HEAD_DIM)
    return kv[:, :, g, 1:3]


def kernel(x_prompt, x_sample, state_conv, cache_kv_w128, cache_kv_w512, cache_kv_w2048, norm_mix, norm_mlp, norm_final, conv_w_pw1, conv_b_pw1, conv_w_dw, conv_b_dw, conv_ln_g, conv_ln_b, conv_w_pw2, conv_b_pw2, attn_w_qkv, attn_w_o, mlp_w1, mlp_w2):
    xp = x_prompt.reshape(NP_ROWS, D_MODEL)
    xs = x_sample.reshape(NS_ROWS, D_MODEL)

    w_pw2 = conv_w_pw2.astype(BF16)
    w_qkv = attn_w_qkv.astype(BF16)
    w_o = attn_w_o.astype(BF16)
    b_pw1 = conv_b_pw1[:, None, :]
    b_dw = conv_b_dw[:, None, :]
    ln_g = conv_ln_g[:, None, :]
    ln_b = conv_ln_b[:, None, :]
    b_pw2 = conv_b_pw2[:, None, :]
    g_final = norm_final[None, :]

    conv_p, conv_s = [], []
    kv_p = [[] for _ in range(N_GROUPS)]
    kv_s = [[] for _ in range(N_GROUPS)]
    for i in range(DEPTH):
        j = i // 2
        g_mix = norm_mix[i][None, :]
        g_mlp = norm_mlp[i][None, :]
        if i % 2 == 0:
            glu_p, glu_s = _pw1_glu(xp, xs, g_mix, conv_w_pw1, b_pw1, j)
            xp_new = _conv_prompt(glu_p, xp, conv_w_dw, b_dw, ln_g, ln_b, w_pw2, b_pw2, j)
            xs, new_state = _conv_sample(glu_s, state_conv, xs, conv_w_dw, b_dw, ln_g, ln_b, w_pw2, b_pw2, j)
            xp = xp_new
            conv_p.append(glu_p.reshape(BATCH, SEQ, D_MODEL)[:, SEQ - CONV_STATE:])
            conv_s.append(new_state)
        else:
            qkv_t, qkv_s, *kv_prompt = _qkv(xp, xs, g_mix, w_qkv, j)
            attn_p = _attn_prompt(qkv_t)
            attn_s = _attn_sample(qkv_s, cache_kv_w128, cache_kv_w512, cache_kv_w2048, j)
            xp, xs = _proj(attn_p, attn_s, xp, xs, w_o, j)
            for g in range(N_GROUPS):
                kv_p[g].append(kv_prompt[g])
                kv_s[g].append(_sample_kv(qkv_s, g))
        xp, xs = _mlp(xp, xs, g_mlp, mlp_w1, mlp_w2, g_final, i, final=(i == DEPTH - 1))

    return (xp.reshape(BATCH, SEQ, D_MODEL), xs.reshape(DEC_BATCH, DEC_SEQ, D_MODEL),
            jnp.stack(conv_p), jnp.stack(conv_s),
            jnp.stack(kv_p[0]), jnp.stack(kv_s[0]),
            jnp.stack(kv_p[1]), jnp.stack(kv_s[1]),
            jnp.stack(kv_p[2]), jnp.stack(kv_s[2]))
```
